```python
import jax, jax.numpy as jnp
from jax import lax
import numpy as np

D_MODEL = 1024
BATCH = 8
SEQ = 2048
DEPTH = 1
DEC_BATCH = 32
DEC_SEQ = 1
PAST_LEN = 16384
PAGE_SIZE = 128

SB_HEADS = 8
SB_HEAD_DIM = 64
SB_WIDTH = SB_HEADS * SB_HEAD_DIM
SB_BLOCK = 128
SB_BIAS_HI = -2.0
SB_BIAS_LO = -10.0
GLA_HEADS = 4
GLA_DK = 64
GLA_DV = 128
GLA_KW = GLA_HEADS * GLA_DK
GLA_VW = GLA_HEADS * GLA_DV
GLA_GATE_RANK = 16
GLA_GATE_TAU = 16.0
GLA_CHUNK = 64
PEER_HEADS = 8
PEER_N_KEYS = 128
PEER_N_EXPERTS = PEER_N_KEYS * PEER_N_KEYS
PEER_TOPK = 16
PEER_QDIM = 256
PEER_HALF = PEER_QDIM // 2
PEER_BLOCK = 256
PLE_DIM = 256
RMS_EPS = 1e-6

_IN_SIZES = (SB_WIDTH, SB_WIDTH, SB_WIDTH,
             GLA_KW, GLA_KW, GLA_VW, GLA_VW,
             GLA_GATE_RANK,
             D_MODEL, D_MODEL)
IN_COLS = sum(_IN_SIZES)
_IN_SPLITS = tuple(int(s) for s in np.cumsum(_IN_SIZES)[:-1])

kernel_name = 'hybrid_stickbreak_gla_peer_step'


def rmsnorm(x, g):
    xf = x.astype(jnp.float32)
    y = xf * lax.rsqrt(jnp.mean(xf * xf, axis=-1, keepdims=True) + RMS_EPS)
    return (y * g.astype(jnp.float32)).astype(x.dtype)


def sb_attend(q, k_segs, v_segs, q_off, k_pos, bias):
    b, tq, h, d = q.shape
    qb = min(SB_BLOCK, tq)
    nq = -(-tq // qb)
    qp = jnp.pad(q, ((0, 0), (0, nq * qb - tq), (0, 0), (0, 0)))
    q_blocks = qp.reshape(b, nq, qb, h, d).transpose(1, 0, 2, 3, 4)
    pos_blocks = (q_off + jnp.arange(nq * qb)).reshape(nq, qb)
    split_pts = [int(s) for s in np.cumsum([kk.shape[1] for kk in k_segs])[:-1]]
    scale = d ** -0.5
    bias_f = bias.astype(jnp.float32)[None, :, None, None]

    def one_block(args):
        qblk, pblk = args
        z = jnp.concatenate([jnp.einsum('bqhd,bkhd->bhqk', qblk, kk) for kk in k_segs],
                            axis=-1).astype(jnp.float32) * scale + bias_f
        mask = k_pos[None, :] < pblk[:, None]
        log_beta = jax.nn.log_sigmoid(z)
        log_keep = jnp.where(mask, jax.nn.log_sigmoid(-z), 0.0)
        later = lax.cumsum(log_keep, axis=3, reverse=True) - log_keep
        w = jnp.where(mask, jnp.exp(log_beta + later), 0.0).astype(qblk.dtype)
        ws = jnp.split(w, split_pts, axis=-1)
        return sum(jnp.einsum('bhqk,bkhd->bqhd', wi, vi) for wi, vi in zip(ws, v_segs))

    out = lax.map(one_block, (q_blocks, pos_blocks))
    return out.transpose(1, 0, 2, 3, 4).reshape(b, nq * qb, h, d)[:, :tq]


def gla_scan(q, k, v, log_a, s0):
    b, t, h, dk = q.shape
    dv = v.shape[-1]
    c = min(GLA_CHUNK, t)
    n = -(-t // c)
    pad = ((0, 0), (0, n * c - t), (0, 0), (0, 0))

    def blocks(a):
        a = jnp.pad(a.astype(jnp.float32), pad)
        return a.reshape(b, n, c, h, a.shape[-1]).transpose(1, 0, 3, 2, 4)

    causal = jnp.tril(jnp.ones((c, c), dtype=bool))

    def step(s, inp):
        qc, kc, vc, ac = inp
        cum = jnp.cumsum(ac, axis=2)
        o = jnp.einsum('bhtd,bhde->bhte', qc * jnp.exp(cum), s)
        diff = jnp.where(causal[:, :, None], cum[:, :, :, None, :] - cum[:, :, None, :, :], -jnp.inf)
        scores = jnp.einsum('bhtd,bhtsd,bhsd->bhts', qc, jnp.exp(diff), kc)
        o = o + jnp.einsum('bhts,bhse->bhte', scores, vc)
        last = cum[:, :, -1:, :]
        s = jnp.exp(last[:, :, 0, :])[..., None] * s + jnp.einsum('bhsd,bhse->bhde', kc * jnp.exp(last - cum), vc)
        return s, o

    s_fin, o = lax.scan(step, s0.astype(jnp.float32), (blocks(q), blocks(k), blocks(v), blocks(log_a)))
    o = o.transpose(1, 0, 3, 2, 4).reshape(b, n * c, h, dv)[:, :t]
    return o.astype(v.dtype), s_fin


def peer(x, w_q, sub_keys, expert_u, expert_v):
    t, d = x.shape
    blk = min(PEER_BLOCK, t)
    nb = -(-t // blk)
    xb = jnp.pad(x, ((0, nb * blk - t), (0, 0))).reshape(nb, blk, d)

    def one_block(xc):
        q = (xc @ w_q).reshape(blk, PEER_HEADS, 2, PEER_HALF)
        s = jnp.einsum('thcd,hcnd->thcn', q, sub_keys).astype(jnp.float32)
        s1, i1 = lax.top_k(s[:, :, 0], PEER_TOPK)
        s2, i2 = lax.top_k(s[:, :, 1], PEER_TOPK)
        cand = (s1[..., :, None] + s2[..., None, :]).reshape(blk, PEER_HEADS, PEER_TOPK * PEER_TOPK)
        top_s, top_c = lax.top_k(cand, PEER_TOPK)
        e = (jnp.take_along_axis(i1, top_c // PEER_TOPK, axis=-1) * PEER_N_KEYS
             + jnp.take_along_axis(i2, top_c % PEER_TOPK, axis=-1))
        g = jax.nn.softmax(top_s, axis=-1)
        act = jax.nn.gelu(jnp.einsum('td,thkd->thk', xc, expert_u[e]).astype(jnp.float32), approximate=False)
        return jnp.einsum('thk,thkd->td', (g * act).astype(xc.dtype), expert_v[e])

    y = lax.map(one_block, xb)
    return y.reshape(nb * blk, d)[:t]


def trunk_layer(h, p, q_off, past_k, past_v, s0, norm1_g, w_in, b_gate, qa_norm_g, ka_norm_g, sb_bias,
                w_alpha, b_alpha, ob_norm_g, w_up_a, w_up_b, w_o, norm2_g, w_peer_q, peer_sub_keys,
                expert_u, expert_v, norm3_g, w_ple, w_ple_gate):
    bsz, t, _ = h.shape
    xn = rmsnorm(h, norm1_g)
    z = xn @ w_in
    qa, ka, va, qb, kb, vb, rb, ab, gza, gzb = jnp.split(z, _IN_SPLITS, axis=-1)
    qa = rmsnorm(qa.reshape(bsz, t, SB_HEADS, SB_HEAD_DIM), qa_norm_g)
    ka = rmsnorm(ka.reshape(bsz, t, SB_HEADS, SB_HEAD_DIM), ka_norm_g)
    va = va.reshape(bsz, t, SB_HEADS, SB_HEAD_DIM)
    new_pos = q_off + jnp.arange(t)
    if past_k is None:
        k_segs, v_segs, k_pos = (ka,), (va,), new_pos
    else:
        k_segs, v_segs = (past_k, ka), (past_v, va)
        k_pos = jnp.concatenate([jnp.arange(past_k.shape[1]), new_pos])
    ya = sb_attend(qa, k_segs, v_segs, q_off, k_pos, sb_bias).reshape(bsz, t, SB_WIDTH)
    qb = qb.reshape(bsz, t, GLA_HEADS, GLA_DK) * (GLA_DK ** -0.5)
    kb = kb.reshape(bsz, t, GLA_HEADS, GLA_DK)
    vb = vb.reshape(bsz, t, GLA_HEADS, GLA_DV)
    log_a = (jax.nn.log_sigmoid((ab @ w_alpha + b_alpha).astype(jnp.float32)) / GLA_GATE_TAU
             ).reshape(bsz, t, GLA_HEADS, GLA_DK)
    ob, s_new = gla_scan(qb, kb, vb, log_a, s0)
    yb = rmsnorm(ob, ob_norm_g).reshape(bsz, t, GLA_VW) * jax.nn.silu(rb)
    ga = jax.nn.sigmoid(gza + b_gate[0])
    gb = jax.nn.sigmoid(gzb + b_gate[1])
    h = h + (ga * (ya @ w_up_a) + gb * (yb @ w_up_b)) @ w_o
    h = h + peer(rmsnorm(h, norm2_g).reshape(bsz * t, D_MODEL), w_peer_q, peer_sub_keys,
                 expert_u, expert_v).reshape(bsz, t, D_MODEL)
    h = h + (p @ w_ple) * jax.nn.sigmoid(rmsnorm(h, norm3_g) @ w_ple_gate)
    return h, ka, va, s_new


def setup_inputs(seed: int = 0) -> dict:
    key = jax.random.key(seed)
    ks = jax.random.split(key, 32)
    f = jnp.float32
    n_pages = PAST_LEN // PAGE_SIZE
    n_used = DEC_BATCH * n_pages
    n_pool = n_used + n_used // 4

    def nrm(k, shape, scale):
        return jax.random.normal(k, shape, f) * scale

    def gain(k, shape):
        return 1.0 + 0.02 * jax.random.normal(k, shape, f)

    page_table = jax.random.permutation(ks[7], n_pool)[:n_used].reshape(DEC_BATCH, n_pages).astype(jnp.int32)
    sb_bias = (jnp.linspace(SB_BIAS_HI, SB_BIAS_LO, SB_HEADS, dtype=f)[None, :]
               + nrm(ks[27], (DEPTH, SB_HEADS), 0.1))
    return {
        'x_prompt': nrm(ks[0], (BATCH, SEQ, D_MODEL), 1.0),
        'x_sample': nrm(ks[1], (DEC_BATCH, DEC_SEQ, D_MODEL), 1.0),
        'p_prompt': nrm(ks[2], (DEPTH, BATCH, SEQ, PLE_DIM), 1.0),
        'p_sample': nrm(ks[3], (DEPTH, DEC_BATCH, DEC_SEQ, PLE_DIM), 1.0),
        'cache_k': nrm(ks[4], (DEPTH, n_pool, PAGE_SIZE, SB_HEADS, SB_HEAD_DIM), 1.0),
        'cache_v': nrm(ks[5], (DEPTH, n_pool, PAGE_SIZE, SB_HEADS, SB_HEAD_DIM), 1.0),
        'state_gla': nrm(ks[6], (DEPTH, DEC_BATCH, GLA_HEADS, GLA_DK, GLA_DV), 0.3),
        'page_table': page_table,
        'norm1_g': gain(ks[8], (DEPTH, D_MODEL)),
        'w_in': nrm(ks[9], (DEPTH, D_MODEL, IN_COLS), D_MODEL ** -0.5),
        'b_gate': nrm(ks[10], (DEPTH, 2, D_MODEL), 0.1),
        'qa_norm_g': gain(ks[11], (DEPTH, SB_HEAD_DIM)),
        'ka_norm_g': gain(ks[12], (DEPTH, SB_HEAD_DIM)),
        'sb_bias': sb_bias,
        'w_alpha': nrm(ks[13], (DEPTH, GLA_GATE_RANK, GLA_KW), GLA_GATE_RANK ** -0.5),
        'b_alpha': nrm(ks[14], (DEPTH, GLA_KW), 0.1),
        'ob_norm_g': gain(ks[15], (DEPTH, GLA_DV)),
        'w_up_a': nrm(ks[16], (DEPTH, SB_WIDTH, D_MODEL), SB_WIDTH ** -0.5),
        'w_up_b': nrm(ks[17], (DEPTH, GLA_VW, D_MODEL), GLA_VW ** -0.5),
        'w_o': nrm(ks[18], (DEPTH, D_MODEL, D_MODEL), D_MODEL ** -0.5),
        'norm2_g': gain(ks[19], (DEPTH, D_MODEL)),
        'w_peer_q': nrm(ks[20], (DEPTH, D_MODEL, PEER_HEADS * PEER_QDIM), D_MODEL ** -0.5),
        'peer_sub_keys': nrm(ks[21], (DEPTH, PEER_HEADS, 2, PEER_N_KEYS, PEER_HALF), PEER_HALF ** -0.5),
        'expert_u': nrm(ks[22], (DEPTH, PEER_N_EXPERTS, D_MODEL), D_MODEL ** -0.5),
        'expert_v': nrm(ks[23], (DEPTH, PEER_N_EXPERTS, D_MODEL), 0.25),
        'norm3_g': gain(ks[24], (DEPTH, D_MODEL)),
        'w_ple': nrm(ks[25], (DEPTH, PLE_DIM, D_MODEL), PLE_DIM ** -0.5),
        'w_ple_gate': nrm(ks[26], (DEPTH, D_MODEL, D_MODEL), D_MODEL ** -0.5),
    }


def reference(x_prompt, x_sample, p_prompt, p_sample, cache_k, cache_v, state_gla, page_table,
              norm1_g, w_in, b_gate, qa_norm_g, ka_norm_g, sb_bias, w_alpha, b_alpha, ob_norm_g, w_up_a,
              w_up_b, w_o, norm2_g, w_peer_q, peer_sub_keys, expert_u, expert_v, norm3_g, w_ple, w_ple_gate):
    dec_b, n_pages = page_table.shape
    past_len = n_pages * PAGE_SIZE
    hp, hs = x_prompt, x_sample
    kp_l, vp_l, sp_l, ks_l, vs_l, ss_l = [], [], [], [], [], []
    for i in range(DEPTH):
        lw = (norm1_g[i], w_in[i], b_gate[i], qa_norm_g[i], ka_norm_g[i], sb_bias[i], w_alpha[i], b_alpha[i],
              ob_norm_g[i], w_up_a[i], w_up_b[i], w_o[i], norm2_g[i], w_peer_q[i], peer_sub_keys[i],
              expert_u[i], expert_v[i], norm3_g[i], w_ple[i], w_ple_gate[i])
        s0 = jnp.zeros((hp.shape[0], GLA_HEADS, GLA_DK, GLA_DV), jnp.float32)
        hp, kp, vp, sp = trunk_layer(hp, p_prompt[i], 0, None, None, s0, *lw)
        past_k = cache_k[i][page_table].reshape(dec_b, past_len, SB_HEADS, SB_HEAD_DIM)
        past_v = cache_v[i][page_table].reshape(dec_b, past_len, SB_HEADS, SB_HEAD_DIM)
        hs, ksn, vsn, ssn = trunk_layer(hs, p_sample[i], past_len, past_k, past_v, state_gla[i], *lw)
        kp_l.append(kp); vp_l.append(vp); sp_l.append(sp)
        ks_l.append(ksn); vs_l.append(vsn); ss_l.append(ssn)
    return (hp, hs, jnp.stack(kp_l), jnp.stack(vp_l), jnp.stack(sp_l),
            jnp.stack(ks_l), jnp.stack(vs_l), jnp.stack(ss_l))
```

```python
import functools

import numpy as np
import jax
import jax.numpy as jnp
from jax import lax
from jax.experimental import pallas as pl
from jax.experimental.pallas import tpu as pltpu

F32 = jnp.float32
BF16 = jnp.bfloat16

D_MODEL = 1024
SB_HEADS = 8
SB_HEAD_DIM = 64
SB_WIDTH = SB_HEADS * SB_HEAD_DIM
PAGE_SIZE = 128
GLA_HEADS = 4
GLA_DK = 64
GLA_DV = 128
GLA_KW = GLA_HEADS * GLA_DK
GLA_VW = GLA_HEADS * GLA_DV
GLA_GATE_RANK = 16
GLA_GATE_TAU = 16.0
GLA_CHUNK = 64
PEER_HEADS = 8
PEER_N_KEYS = 128
PEER_TOPK = 16
PEER_HALF = 128
PLE_DIM = 256
RMS_EPS = 1e-6

LANES = 128
VMEM_LIMIT = 56 * 1024 * 1024
NEG_INF = float("-inf")


def _cparams(sem):
    return pltpu.CompilerParams(dimension_semantics=sem, vmem_limit_bytes=VMEM_LIMIT)


def _dot(a, b):
    return jnp.dot(a, b, preferred_element_type=F32)


def _dot_nt(a, b):
    return lax.dot_general(a, b, (((1,), (1,)), ((), ())), preferred_element_type=F32)


def _dot_tn(a, b):
    return lax.dot_general(a, b, (((0,), (0,)), ((), ())), preferred_element_type=F32)


def _split_bf16(x):
    hi = x.astype(BF16)
    lo = (x - hi.astype(F32)).astype(BF16)
    return hi, lo


def _rms_rows(x, g):
    return x * lax.rsqrt(jnp.mean(x * x, axis=-1, keepdims=True) + RMS_EPS) * g


def _log_sigmoid(z):
    return jnp.minimum(z, 0.0) - jnp.log1p(jnp.exp(-jnp.abs(z)))


def _sigmoid(z):
    return 1.0 / (1.0 + jnp.exp(-z))


def _inproj_kernel(x_ref, g1_ref, wa_ref, wb_ref, wab_ref, gq_ref, gk_ref, hm_ref, wal_ref, bal_ref,
                   q_ref, k_ref, k16_ref, v_ref, v16_ref, qb_ref, kb_ref, vb_ref, rb_ref, la_ref):
    xb = _rms_rows(x_ref[...], g1_ref[...]).astype(BF16)
    za = _dot(xb, wa_ref[...])
    qa, ka, va = za[:, :SB_WIDTH], za[:, SB_WIDTH:2 * SB_WIDTH], za[:, 2 * SB_WIDTH:]

    def head_norm(u, g):
        ms = _dot((u * u).astype(BF16), hm_ref[...])
        return u * lax.rsqrt(ms + RMS_EPS) * g

    qn = head_norm(qa, gq_ref[...]) * (SB_HEAD_DIM ** -0.5)
    kn = head_norm(ka, gk_ref[...])
    q_ref[...] = qn.astype(BF16)
    k_ref[...] = kn
    k16_ref[...] = kn.astype(BF16)
    v_ref[...] = va
    v16_ref[...] = va.astype(BF16)

    zb = _dot(xb, wb_ref[...])
    qb_ref[...] = zb[:, :GLA_KW] * (GLA_DK ** -0.5)
    kb_ref[...] = zb[:, GLA_KW:2 * GLA_KW]
    vb_ref[...] = zb[:, 2 * GLA_KW:2 * GLA_KW + GLA_VW].astype(BF16)
    rb = zb[:, 2 * GLA_KW + GLA_VW:]
    rb_ref[...] = rb * _sigmoid(rb)
    ab = _dot(xb, wab_ref[...])
    gl = _dot(ab.astype(BF16), wal_ref[...]) + bal_ref[...]
    la_ref[...] = _log_sigmoid(gl) * (1.0 / GLA_GATE_TAU)


def _inproj(x, w, tm):
    n = x.shape[0]
    row = lambda c: pl.BlockSpec((tm, c), lambda i: (i, 0))
    full = lambda a: pl.BlockSpec(a.shape, lambda i: (0,) * a.ndim)
    ins = [x, w["g1"], w["wa"], w["wb"], w["wab"], w["gq"], w["gk"], w["hm"], w["wal"], w["bal"]]
    outs = [(SB_WIDTH, BF16), (SB_WIDTH, F32), (SB_WIDTH, BF16), (SB_WIDTH, F32), (SB_WIDTH, BF16),
            (GLA_KW, F32), (GLA_KW, F32), (GLA_VW, BF16), (GLA_VW, F32), (GLA_KW, F32)]
    return pl.pallas_call(
        _inproj_kernel,
        grid=(n // tm,),
        in_specs=[row(D_MODEL)] + [full(a) for a in ins[1:]],
        out_specs=[row(c) for c, _ in outs],
        out_shape=[jax.ShapeDtypeStruct((n, c), d) for c, d in outs],
        compiler_params=_cparams(("parallel",)),
        name="inproj",
    )(*ins)


def _sb_block(qh, k, v, bias, tri, carry, acc, mask):
    z = _dot_nt(qh, k) + bias
    lb = _log_sigmoid(z)
    lk = lb - z
    if mask is not None:
        lk = jnp.where(mask, lk, 0.0)
    hi, lo = _split_bf16(lk)
    r = _dot(jnp.concatenate([hi, lo], axis=1), tri)
    w = jnp.exp(lb + r[:, :LANES] + carry)
    if mask is not None:
        w = jnp.where(mask, w, 0.0)
    acc = acc + _dot(w.astype(BF16), v)
    return carry + r[:, LANES:], acc


def _sb_prompt_kernel(bias_ref, q_ref, k_ref, v_ref, tri_ref, o_ref):
    pair = pl.program_id(1)
    qi = pl.program_id(2)
    tq = q_ref.shape[1]
    q2 = q_ref[0]
    tri = tri_ref[...]
    lane = lax.broadcasted_iota(jnp.int32, (tq, LANES), 1)
    rowi = lax.broadcasted_iota(jnp.int32, (tq, LANES), 0)
    causal = lane < rowi
    low = lane < SB_HEAD_DIM
    accs = []
    for hh in range(2):
        qh = jnp.where(low if hh == 0 else jnp.logical_not(low), q2, jnp.zeros_like(q2))
        bias = bias_ref[2 * pair + hh]
        zero = jnp.zeros((tq, LANES), F32)
        kd = k_ref[0, pl.ds(pl.multiple_of(qi * tq, tq), tq), :]
        vd = v_ref[0, pl.ds(pl.multiple_of(qi * tq, tq), tq), :]
        carry, acc = _sb_block(qh, kd, vd, bias, tri, zero, zero, causal)

        def body(i, c):
            kb = qi - 1 - i
            start = pl.multiple_of(kb * tq, tq)
            return _sb_block(qh, k_ref[0, pl.ds(start, tq), :], v_ref[0, pl.ds(start, tq), :],
                             bias, tri, c[0], c[1], None)

        carry, acc = lax.fori_loop(0, qi, body, (carry, acc))
        accs.append(acc)
    o_ref[0] = jnp.where(low, accs[0], accs[1]).astype(o_ref.dtype)


def _sb_prompt(q, k, v, bias, tri):
    b, t, _ = q.shape
    tq = LANES
    pairs = SB_HEADS // 2
    grid_spec = pltpu.PrefetchScalarGridSpec(
        num_scalar_prefetch=1,
        grid=(b, pairs, t // tq),
        in_specs=[pl.BlockSpec((1, tq, LANES), lambda bi, p, qi, s: (bi, qi, p)),
                  pl.BlockSpec((1, t, LANES), lambda bi, p, qi, s: (bi, 0, p)),
                  pl.BlockSpec((1, t, LANES), lambda bi, p, qi, s: (bi, 0, p)),
                  pl.BlockSpec(tri.shape, lambda bi, p, qi, s: (0, 0))],
        out_specs=pl.BlockSpec((1, tq, LANES), lambda bi, p, qi, s: (bi, qi, p)),
    )
    return pl.pallas_call(
        _sb_prompt_kernel,
        grid_spec=grid_spec,
        out_shape=jax.ShapeDtypeStruct((b, t, SB_WIDTH), BF16),
        compiler_params=_cparams(("parallel", "parallel", "arbitrary")),
        name="sb_prompt",
    )(bias, q, k, v, tri)


PAGES_PER_STEP = 8


def _sb_decode_kernel(pt_ref, q_ref, bias_ref, tri_ref, pmat_ref, *refs):
    npg = PAGES_PER_STEP
    k_refs, v_refs = refs[:npg], refs[npg:2 * npg]
    o_ref, c_ref, acc_ref = refs[2 * npg:]
    step = pl.program_id(1)

    @pl.when(step == 0)
    def _():
        c_ref[...] = jnp.zeros_like(c_ref)
        acc_ref[...] = jnp.zeros_like(acc_ref)

    hd = SB_HEADS
    lane = lax.broadcasted_iota(jnp.int32, (hd, SB_WIDTH), 1)
    sub = lax.broadcasted_iota(jnp.int32, (hd, SB_WIDTH), 0)
    own = (lane // SB_HEAD_DIM) == sub
    qrow = q_ref[0].astype(F32)
    qm = jnp.where(own, jnp.broadcast_to(qrow, (hd, SB_WIDTH)), 0.0).astype(BF16)
    z = jnp.concatenate([_dot_nt(qm, kr[0].astype(BF16)) for kr in k_refs], axis=0) + bias_ref[...]
    lb = _log_sigmoid(z)
    lk = lb - z
    hi, lo = _split_bf16(lk)
    r = _dot(jnp.concatenate([hi, lo], axis=1), tri_ref[...])
    tot = r[:, LANES:]
    thi, tlo = _split_bf16(tot)
    within = _dot(pmat_ref[...], jnp.concatenate([thi, tlo], axis=0))
    c = c_ref[...]
    later = r[:, :LANES] + within + jnp.concatenate([c] * npg, axis=0)
    w = jnp.exp(lb + later).astype(BF16)
    acc = acc_ref[...]
    for i in range(npg):
        acc = acc + _dot(w[i * hd:(i + 1) * hd], v_refs[i][0].astype(BF16))
        c = c + tot[i * hd:(i + 1) * hd]
    acc_ref[...] = acc
    c_ref[...] = c

    @pl.when(step == pl.num_programs(1) - 1)
    def _():
        o_ref[0] = jnp.sum(jnp.where(own, acc, 0.0), axis=0, keepdims=True).astype(o_ref.dtype)


def _sb_decode(q, cache_k, cache_v, page_table, bias_tile, tri, pmat):
    s, n_pages = page_table.shape
    npg = PAGES_PER_STEP
    steps = n_pages // npg

    def page_spec(i):
        return pl.BlockSpec((1, PAGE_SIZE, SB_WIDTH),
                            lambda si, j, pt: (pt[si, n_pages - 1 - (npg * j + i)], 0, 0))

    const = lambda a: pl.BlockSpec(a.shape, lambda si, j, pt: (0,) * a.ndim)
    grid_spec = pltpu.PrefetchScalarGridSpec(
        num_scalar_prefetch=1,
        grid=(s, steps),
        in_specs=[pl.BlockSpec((1, 1, SB_WIDTH), lambda si, j, pt: (si, 0, 0)),
                  const(bias_tile), const(tri), const(pmat)]
                 + [page_spec(i) for i in range(npg)] + [page_spec(i) for i in range(npg)],
        out_specs=pl.BlockSpec((1, 1, SB_WIDTH), lambda si, j, pt: (si, 0, 0)),
        scratch_shapes=[pltpu.VMEM((SB_HEADS, LANES), F32), pltpu.VMEM((SB_HEADS, SB_WIDTH), F32)],
    )
    return pl.pallas_call(
        _sb_decode_kernel,
        grid_spec=grid_spec,
        out_shape=jax.ShapeDtypeStruct((s, 1, SB_WIDTH), BF16),
        compiler_params=_cparams(("parallel", "arbitrary")),
        name="sb_decode",
    )(page_table, q, bias_tile, tri, pmat, *([cache_k] * npg), *([cache_v] * npg))


GLA_LEVELS = 6
GLA_MROWS = (2 * GLA_LEVELS + 2) * GLA_CHUNK


def _gla_constants():
    c = GLA_CHUNK
    j = np.arange(c)[None, :]
    t = np.arange(c)[:, None]
    blocks = [(j <= t)]
    masks = []
    qs, ks = [], []
    for lv in range(GLA_LEVELS):
        h = c >> (lv + 1)
        mid = (t // (2 * h)) * (2 * h) + h
        upper = (t % (2 * h)) >= h
        qs.append(upper & (j >= mid) & (j <= t))
        ks.append((~upper) & (j > t) & (j <= mid - 1))
        s = np.arange(c)[None, :]
        masks.append(((t // (2 * h)) == (s // (2 * h))) & upper & ((s % (2 * h)) < h))
    blocks += qs + ks + [(j > t)]
    masks.append(np.eye(c, dtype=bool))
    m = np.concatenate(blocks, axis=0).astype(np.float32)
    m2 = np.concatenate([m, m], axis=1)
    return jnp.asarray(m2, BF16), jnp.asarray(np.stack(masks).astype(np.float32))


def _gla_kernel(q_ref, k_ref, v_ref, la_ref, m_ref, lm_ref, o_ref, st_ref, s_scr):
    ci = pl.program_id(1)
    c = GLA_CHUNK

    @pl.when(ci == 0)
    def _():
        s_scr[...] = jnp.zeros_like(s_scr)

    q = q_ref[0]
    k = k_ref[0]
    v = v_ref[0]
    hi, lo = _split_bf16(la_ref[0])
    e = jnp.exp(_dot(m_ref[...], jnp.concatenate([hi, lo], axis=0)))
    blk = lambda i: e[i * c:(i + 1) * c]
    lane = lax.broadcasted_iota(jnp.int32, (c, GLA_KW), 1)
    q_in = (q * blk(0)).astype(BF16)
    k_st = (k * blk(2 * GLA_LEVELS + 1)).astype(BF16)
    qs = [(q * blk(1 + lv)).astype(BF16) for lv in range(GLA_LEVELS)] + [q.astype(BF16)]
    ks = [(k * blk(1 + GLA_LEVELS + lv)).astype(BF16) for lv in range(GLA_LEVELS)] + [k.astype(BF16)]
    s_old = s_scr[...]
    s_old16 = s_old.astype(BF16)
    zero16 = jnp.zeros((c, GLA_KW), BF16)
    upd = jnp.zeros_like(s_old)
    for h in range(GLA_HEADS):
        own = (lane // GLA_DK) == h
        vh = v[:, h * GLA_DV:(h + 1) * GLA_DV]
        scores = jnp.zeros((c, c), F32)
        for lv in range(GLA_LEVELS + 1):
            scores = scores + lm_ref[lv] * _dot_nt(jnp.where(own, qs[lv], zero16), ks[lv])
        o = _dot_nt(jnp.where(own, q_in, zero16), s_old16) + _dot(scores.astype(BF16), vh)
        o_ref[0, :, h * GLA_DV:(h + 1) * GLA_DV] = o
        upd = upd + _dot_tn(vh, jnp.where(own, k_st, zero16))
    s_new = s_old * e[c - 1:c, :] + upd
    s_scr[...] = s_new

    @pl.when(ci == pl.num_programs(1) - 1)
    def _():
        st_ref[0] = s_new


def _gla_prompt(q, k, v, la, m2, lmask):
    b, t, _ = q.shape
    c = GLA_CHUNK
    row = lambda w: pl.BlockSpec((1, c, w), lambda bi, ci: (bi, ci, 0))
    const = lambda a: pl.BlockSpec(a.shape, lambda bi, ci: (0,) * a.ndim)
    return pl.pallas_call(
        _gla_kernel,
        grid=(b, t // c),
        in_specs=[row(GLA_KW), row(GLA_KW), row(GLA_VW), row(GLA_KW), const(m2), const(lmask)],
        out_specs=[row(GLA_VW), pl.BlockSpec((1, GLA_DV, GLA_KW), lambda bi, ci: (bi, 0, 0))],
        out_shape=[jax.ShapeDtypeStruct((b, t, GLA_VW), F32),
                   jax.ShapeDtypeStruct((b, GLA_DV, GLA_KW), F32)],
        scratch_shapes=[pltpu.VMEM((GLA_DV, GLA_KW), F32)],
        compiler_params=_cparams(("parallel", "arbitrary")),
        name="gla_prompt",
    )(q, k, v, la, m2, lmask)


def _gla_step_kernel(q_ref, k_ref, la_ref, v_ref, s_ref, o_ref, sn_ref):
    a = jnp.exp(la_ref[0])
    s_new = a * s_ref[0] + k_ref[0] * v_ref[0]
    sn_ref[0] = s_new
    o_ref[0] = jnp.sum(q_ref[0] * s_new, axis=1, keepdims=True)


def _gla_step(q, k, la, v, s0):
    s = q.shape[0]
    col = pl.BlockSpec((1, GLA_HEADS, GLA_DK, 1), lambda i: (i, 0, 0, 0))
    vrow = pl.BlockSpec((1, GLA_HEADS, 1, GLA_DV), lambda i: (i, 0, 0, 0))
    st = pl.BlockSpec((1, GLA_HEADS, GLA_DK, GLA_DV), lambda i: (i, 0, 0, 0))
    return pl.pallas_call(
        _gla_step_kernel,
        grid=(s,),
        in_specs=[col, col, col, vrow, st],
        out_specs=[vrow, st],
        out_shape=[jax.ShapeDtypeStruct((s, GLA_HEADS, 1, GLA_DV), F32),
                   jax.ShapeDtypeStruct((s, GLA_HEADS, GLA_DK, GLA_DV), F32)],
        compiler_params=_cparams(("parallel",)),
        name="gla_step",
    )(q, k, la, v, s0)


def _merge_kernel(x_ref, ya_ref, ob_ref, rb_ref, g1_ref, wg_ref, bg_ref, gob_ref, wua_ref, wub_ref, wo_ref,
                  h_ref):
    x = x_ref[...]
    xb = _rms_rows(x, g1_ref[...]).astype(BF16)
    gates = _sigmoid(_dot(xb, wg_ref[...]) + bg_ref[...])
    ob = ob_ref[...]
    parts = []
    for h in range(GLA_HEADS):
        oh = ob[:, h * GLA_DV:(h + 1) * GLA_DV]
        parts.append(oh * lax.rsqrt(jnp.mean(oh * oh, axis=-1, keepdims=True) + RMS_EPS))
    yb = jnp.concatenate(parts, axis=1) * gob_ref[...] * rb_ref[...]
    m = (gates[:, :D_MODEL] * _dot(ya_ref[...], wua_ref[...])
         + gates[:, D_MODEL:] * _dot(yb.astype(BF16), wub_ref[...]))
    h_ref[...] = x + _dot(m.astype(BF16), wo_ref[...])


def _merge(x, ya, ob, rbs, w, tm):
    n = x.shape[0]
    row = lambda c: pl.BlockSpec((tm, c), lambda i: (i, 0))
    full = lambda a: pl.BlockSpec(a.shape, lambda i: (0,) * a.ndim)
    ws = [w["g1"], w["wg"], w["bg"], w["gob"], w["wua"], w["wub"], w["wo"]]
    return pl.pallas_call(
        _merge_kernel,
        grid=(n // tm,),
        in_specs=[row(D_MODEL), row(SB_WIDTH), row(GLA_VW), row(GLA_VW)] + [full(a) for a in ws],
        out_specs=row(D_MODEL),
        out_shape=jax.ShapeDtypeStruct((n, D_MODEL), F32),
        compiler_params=_cparams(("parallel",)),
        name="merge",
    )(x, ya, ob, rbs, *ws)


def _top_rows(x, n):
    rows = []
    for _ in range(n):
        m = jnp.max(x, axis=0, keepdims=True)
        rows.append(m)
        x = jnp.where(x == m, NEG_INF, x)
    return rows


def _peer_prep_kernel(h_ref, g2_ref, wq_ref, sk_ref, xn_ref, s1_ref, s2_ref, p1_ref, p2_ref, thr_ref):
    k = PEER_TOPK
    xb = _rms_rows(h_ref[...], g2_ref[...]).astype(BF16)
    xn_ref[...] = xb
    qf = _dot(xb, wq_ref[...]).astype(BF16)
    tm = qf.shape[0]
    rowi = {r: lax.broadcasted_iota(jnp.int32, (r, tm), 0) for r in (8, k)}
    for h in range(PEER_HEADS):
        s = [_dot_nt(sk_ref[2 * h + c], qf[:, (2 * h + c) * PEER_HALF:(2 * h + c + 1) * PEER_HALF])
             for c in range(2)]
        v1 = jnp.concatenate(_top_rows(s[0], k), axis=0)
        v2 = _top_rows(s[1], k)
        cands = []
        for b in range(k):
            n_b = k // (b + 1)
            rows = k if b == 0 else 8
            cands.append(jnp.where(rowi[rows] < n_b, v1[:rows] + v2[b], NEG_INF))
        top = _top_rows(jnp.concatenate(cands, axis=0), k)
        zsum = sum(jnp.exp(r - top[0]) for r in top)
        s1_ref[h] = s[0]
        s2_ref[h] = s[1]
        p1_ref[h] = jnp.exp(s[0] - v1[0:1]) / zsum
        p2_ref[h] = jnp.exp(s[1] - v2[0])
        thr_ref[h] = jnp.broadcast_to(top[k - 1], (8, tm))


def _peer_prep(h1, w, tm):
    n = h1.shape[0]
    full = lambda a: pl.BlockSpec(a.shape, lambda i: (0,) * a.ndim)
    hk = pl.BlockSpec((PEER_HEADS, PEER_N_KEYS, tm), lambda i: (0, 0, i))
    hk_shape = jax.ShapeDtypeStruct((PEER_HEADS, PEER_N_KEYS, n), F32)
    return pl.pallas_call(
        _peer_prep_kernel,
        grid=(n // tm,),
        in_specs=[pl.BlockSpec((tm, D_MODEL), lambda i: (i, 0)), full(w["g2"]), full(w["wq"]), full(w["sk"])],
        out_specs=[pl.BlockSpec((tm, D_MODEL), lambda i: (i, 0)), hk, hk, hk, hk,
                   pl.BlockSpec((PEER_HEADS, 8, tm), lambda i: (0, 0, i))],
        out_shape=[jax.ShapeDtypeStruct((n, D_MODEL), BF16), hk_shape, hk_shape, hk_shape, hk_shape,
                   jax.ShapeDtypeStruct((PEER_HEADS, 8, n), F32)],
        compiler_params=_cparams(("parallel",)),
        name="peer_prep",
    )(h1, w["g2"], w["wq"], w["sk"])


PEER_ROWS_PER_STEP = 8


def _gelu(a):
    return 0.5 * a * (1.0 + lax.erf(a * (2.0 ** -0.5)))


def _peer_dense_kernel(xn_ref, u_ref, vt_ref, s1_ref, s2_ref, p1_ref, p2_ref, thr_ref,
                       h_ref, p_ref, g3_ref, wple_ref, wpg_ref, y_ref, acc_ref, wg_ref):
    j = pl.program_id(1)
    tt = xn_ref.shape[0]
    nk = PEER_N_KEYS

    @pl.when(j == 0)
    def _():
        acc_ref[...] = jnp.zeros_like(acc_ref)

    act = _gelu(_dot_nt(u_ref[...], xn_ref[...]))
    rows = pl.ds(pl.multiple_of(j * PEER_ROWS_PER_STEP, PEER_ROWS_PER_STEP), PEER_ROWS_PER_STEP)
    for r in range(PEER_ROWS_PER_STEP):
        for tl in range(tt // LANES):
            ls = slice(tl * LANES, (tl + 1) * LANES)
            wsum = jnp.zeros((nk, LANES), F32)
            for h in range(PEER_HEADS):
                s1b = s1_ref[h, rows, ls][r:r + 1]
                p1b = p1_ref[h, rows, ls][r:r + 1]
                sel = (s2_ref[h, :, ls] + s1b) >= thr_ref[h, 0:1, ls]
                wsum = wsum + jnp.where(sel, p2_ref[h, :, ls], 0.0) * p1b
            wg_ref[r * nk:(r + 1) * nk, ls] = (wsum * act[r * nk:(r + 1) * nk, ls]).astype(BF16)
    acc_ref[...] += _dot(vt_ref[...], wg_ref[...])

    @pl.when(j == pl.num_programs(1) - 1)
    def _():
        h2 = h_ref[...] + acc_ref[...].T
        xb = _rms_rows(h2, g3_ref[...]).astype(BF16)
        gate = _sigmoid(_dot(xb, wpg_ref[...]))
        y_ref[...] = h2 + _dot(p_ref[...].astype(BF16), wple_ref[...]) * gate


def _peer_dense(xn, s1, s2, p1, p2, thr, h1, p, w, tt):
    n = xn.shape[0]
    eb = PEER_ROWS_PER_STEP * PEER_N_KEYS
    n_exp = w["u"].shape[0]
    tok = lambda c: pl.BlockSpec((tt, c), lambda i, j: (i, 0))
    hk = pl.BlockSpec((PEER_HEADS, PEER_N_KEYS, tt), lambda i, j: (0, 0, i))
    full = lambda a: pl.BlockSpec(a.shape, lambda i, j: (0,) * a.ndim)
    return pl.pallas_call(
        _peer_dense_kernel,
        grid=(n // tt, n_exp // eb),
        in_specs=[tok(D_MODEL),
                  pl.BlockSpec((eb, D_MODEL), lambda i, j: (j, 0)),
                  pl.BlockSpec((D_MODEL, eb), lambda i, j: (0, j)),
                  hk, hk, hk, hk,
                  pl.BlockSpec((PEER_HEADS, 8, tt), lambda i, j: (0, 0, i)),
                  tok(D_MODEL), tok(PLE_DIM), full(w["g3"]), full(w["wple"]), full(w["wpg"])],
        out_specs=tok(D_MODEL),
        out_shape=jax.ShapeDtypeStruct((n, D_MODEL), F32),
        scratch_shapes=[pltpu.VMEM((D_MODEL, tt), F32), pltpu.VMEM((eb, tt), BF16)],
        compiler_params=_cparams(("parallel", "arbitrary")),
        name="peer_dense",
    )(xn, w["u"], w["vt"], s1, s2, p1, p2, thr, h1, p, w["g3"], w["wple"], w["wpg"])


def _sb_constants():
    j = np.arange(LANES)[:, None]
    s = np.arange(LANES)[None, :]
    half = np.concatenate([(j > s), np.ones((LANES, LANES), bool)], axis=1)
    tri = np.concatenate([half, half], axis=0).astype(np.float32)
    npg, hd = PAGES_PER_STEP, SB_HEADS
    r = np.arange(npg * hd)
    p = ((r[:, None] % hd) == (r[None, :] % hd)) & ((r[None, :] // hd) < (r[:, None] // hd))
    pmat = np.concatenate([p, p], axis=1).astype(np.float32)
    return jnp.asarray(tri, BF16), jnp.asarray(pmat, BF16)


def _layer_weights(i, norm1_g, w_in, b_gate, qa_norm_g, ka_norm_g, w_alpha, b_alpha, ob_norm_g, w_up_a,
                   w_up_b, w_o, norm2_g, w_peer_q, peer_sub_keys, expert_u, expert_v, norm3_g, w_ple,
                   w_ple_gate):
    win = w_in[i]
    c_a = 3 * SB_WIDTH
    c_b = c_a + 2 * GLA_KW + 2 * GLA_VW
    c_ab = c_b + GLA_GATE_RANK
    head = np.arange(SB_WIDTH) // SB_HEAD_DIM
    hm = (head[:, None] == head[None, :]).astype(np.float32) / SB_HEAD_DIM
    wab = jnp.zeros((D_MODEL, LANES), F32).at[:, :GLA_GATE_RANK].set(win[:, c_b:c_ab])
    wal = jnp.zeros((LANES, GLA_KW), F32).at[:GLA_GATE_RANK].set(w_alpha[i])
    return {
        "g1": norm1_g[i][None, :],
        "wa": win[:, :c_a].astype(BF16),
        "wb": win[:, c_a:c_b].astype(BF16),
        "wab": wab.astype(BF16),
        "gq": jnp.tile(qa_norm_g[i], SB_HEADS)[None, :],
        "gk": jnp.tile(ka_norm_g[i], SB_HEADS)[None, :],
        "hm": jnp.asarray(hm, BF16),
        "wal": wal.astype(BF16),
        "bal": b_alpha[i][None, :],
        "wg": win[:, c_ab:].astype(BF16),
        "bg": b_gate[i].reshape(1, 2 * D_MODEL),
        "gob": jnp.tile(ob_norm_g[i], GLA_HEADS)[None, :],
        "wua": w_up_a[i].astype(BF16),
        "wub": w_up_b[i].astype(BF16),
        "wo": w_o[i].astype(BF16),
        "g2": norm2_g[i][None, :],
        "wq": w_peer_q[i].astype(BF16),
        "sk": peer_sub_keys[i].reshape(2 * PEER_HEADS, PEER_N_KEYS, PEER_HALF).astype(BF16),
        "u": expert_u[i].astype(BF16),
        "vt": expert_v[i].astype(BF16).T,
        "g3": norm3_g[i][None, :],
        "wple": w_ple[i].astype(BF16),
        "wpg": w_ple_gate[i].astype(BF16),
    }


def _pad_rows(a, n):
    return jnp.pad(a, ((0, n - a.shape[0]),) + ((0, 0),) * (a.ndim - 1))


def kernel(x_prompt, x_sample, p_prompt, p_sample, cache_k, cache_v, state_gla, page_table, norm1_g, w_in,
           b_gate, qa_norm_g, ka_norm_g, sb_bias, w_alpha, b_alpha, ob_norm_g, w_up_a, w_up_b, w_o, norm2_g,
           w_peer_q, peer_sub_keys, expert_u, expert_v, norm3_g, w_ple, w_ple_gate):
    depth = w_in.shape[0]
    bsz, seq, _ = x_prompt.shape
    dec_b, dec_t, _ = x_sample.shape
    assert dec_t == 1 and seq % LANES == 0 and page_table.shape[1] % PAGES_PER_STEP == 0
    n_p = bsz * seq
    n_s = LANES
    tri, pmat = _sb_constants()
    gla_m, gla_masks = _gla_constants()

    hp = x_prompt.reshape(n_p, D_MODEL)
    hs = _pad_rows(x_sample.reshape(dec_b, D_MODEL), n_s)
    outs = [[] for _ in range(6)]
    for i in range(depth):
        w = _layer_weights(i, norm1_g, w_in, b_gate, qa_norm_g, ka_norm_g, w_alpha, b_alpha, ob_norm_g,
                           w_up_a, w_up_b, w_o, norm2_g, w_peer_q, peer_sub_keys, expert_u, expert_v,
                           norm3_g, w_ple, w_ple_gate)
        bias = sb_bias[i].astype(F32)

        q, k, k16, v, v16, qb, kb, vb, rbs, la = _inproj(hp, w, 256)
        r3 = lambda a: a.reshape(bsz, seq, a.shape[-1])
        ya = _sb_prompt(r3(q), r3(k16), r3(v16), bias, tri)
        ob, st = _gla_prompt(r3(qb), r3(kb), r3(vb), r3(la), gla_m, gla_masks)
        h1 = _merge(hp, ya.reshape(n_p, SB_WIDTH), ob.reshape(n_p, GLA_VW), rbs, w, 256)
        xn, s1, s2, p1, p2, thr = _peer_prep(h1, w, 256)
        hp = _peer_dense(xn, s1, s2, p1, p2, thr, h1, p_prompt[i].reshape(n_p, PLE_DIM), w, 512)
        outs[0].append(k.reshape(bsz, seq, SB_HEADS, SB_HEAD_DIM))
        outs[1].append(v.reshape(bsz, seq, SB_HEADS, SB_HEAD_DIM))
        outs[2].append(st.reshape(bsz, GLA_DV, GLA_HEADS, GLA_DK).transpose(0, 2, 3, 1))

        q, k, k16, v, v16, qb, kb, vb, rbs, la = _inproj(hs, w, n_s)
        n_pool = cache_k.shape[1]
        ck = cache_k[i].reshape(n_pool, PAGE_SIZE, SB_WIDTH)
        cv = cache_v[i].reshape(n_pool, PAGE_SIZE, SB_WIDTH)
        bias_tile = jnp.broadcast_to(jnp.tile(bias, PAGES_PER_STEP)[:, None], (PAGES_PER_STEP * SB_HEADS, LANES))
        ya = _sb_decode(q[:dec_b].reshape(dec_b, 1, SB_WIDTH), ck, cv, page_table, bias_tile, tri, pmat)
        col = lambda a: a[:dec_b].reshape(dec_b, GLA_HEADS, GLA_DK, 1)
        ob, st = _gla_step(col(qb), col(kb), col(la),
                           vb[:dec_b].astype(F32).reshape(dec_b, GLA_HEADS, 1, GLA_DV), state_gla[i])
        h1 = _merge(hs, _pad_rows(ya.reshape(dec_b, SB_WIDTH), n_s), _pad_rows(ob.reshape(dec_b, GLA_VW), n_s),
                    rbs, w, n_s)
        xn, s1, s2, p1, p2, thr = _peer_prep(h1, w, n_s)
        hs = _peer_dense(xn, s1, s2, p1, p2, thr, h1, _pad_rows(p_sample[i].reshape(dec_b, PLE_DIM), n_s), w, n_s)
        outs[3].append(k[:dec_b].reshape(dec_b, 1, SB_HEADS, SB_HEAD_DIM))
        outs[4].append(v[:dec_b].reshape(dec_b, 1, SB_HEADS, SB_HEAD_DIM))
        outs[5].append(st)

    return (hp.reshape(bsz, seq, D_MODEL), hs[:dec_b].reshape(dec_b, 1, D_MODEL),
            jnp.stack(outs[0]), jnp.stack(outs[1]), jnp.stack(outs[2]),
            jnp.stack(outs[3]), jnp.stack(outs[4]), jnp.stack(outs[5]))
```

```python
import functools

import numpy as np
import jax
import jax.numpy as jnp
from jax import lax
from jax.experimental import pallas as pl
from jax.experimental.pallas import tpu as pltpu

F32 = jnp.float32
BF16 = jnp.bfloat16

D_MODEL = 1024
SB_HEADS = 8
SB_HEAD_DIM = 64
SB_WIDTH = SB_HEADS * SB_HEAD_DIM
PAGE_SIZE = 128
GLA_HEADS = 4
GLA_DK = 64
GLA_DV = 128
GLA_KW = GLA_HEADS * GLA_DK
GLA_VW = GLA_HEADS * GLA_DV
GLA_GATE_RANK = 16
GLA_GATE_TAU = 16.0
GLA_CHUNK = 64
PEER_HEADS = 8
PEER_N_KEYS = 128
PEER_TOPK = 16
PEER_HALF = 128
PLE_DIM = 256
RMS_EPS = 1e-6

LANES = 128
VMEM_LIMIT = 56 * 1024 * 1024
NEG_INF = float("-inf")
LOG2E = 1.4426950408889634


def _cparams(sem):
    return pltpu.CompilerParams(dimension_semantics=sem, vmem_limit_bytes=VMEM_LIMIT)


def _dot(a, b):
    return jnp.dot(a, b, preferred_element_type=F32)


def _dot_nt(a, b):
    return lax.dot_general(a, b, (((1,), (1,)), ((), ())), preferred_element_type=F32)


def _dot_tn(a, b):
    return lax.dot_general(a, b, (((0,), (0,)), ((), ())), preferred_element_type=F32)


def _split_bf16(x):
    hi = x.astype(BF16)
    lo = (x - hi.astype(F32)).astype(BF16)
    return hi, lo


def _rms_rows(x, g):
    return x * lax.rsqrt(jnp.mean(x * x, axis=-1, keepdims=True) + RMS_EPS) * g


def _log_sigmoid(z):
    return jnp.minimum(z, 0.0) - jnp.log1p(jnp.exp(-jnp.abs(z)))


def _sigmoid(z):
    return 1.0 / (1.0 + jnp.exp(-z))


def _inproj_kernel(x_ref, g1_ref, wa_ref, wb_ref, wab_ref, gq_ref, gk_ref, hm_ref, wal_ref, bal_ref,
                   q_ref, k_ref, k16_ref, v_ref, v16_ref, qb_ref, kb_ref, vb_ref, rb_ref, la_ref):
    xb = _rms_rows(x_ref[...], g1_ref[...]).astype(BF16)
    za = _dot(xb, wa_ref[...])
    qa, ka, va = za[:, :SB_WIDTH], za[:, SB_WIDTH:2 * SB_WIDTH], za[:, 2 * SB_WIDTH:]

    def head_norm(u, g):
        ms = _dot((u * u).astype(BF16), hm_ref[...])
        return u * lax.rsqrt(ms + RMS_EPS) * g

    qn = head_norm(qa, gq_ref[...]) * (SB_HEAD_DIM ** -0.5 * LOG2E)
    kn = head_norm(ka, gk_ref[...])
    q_ref[...] = qn.astype(BF16)
    k_ref[...] = kn
    k16_ref[...] = kn.astype(BF16)
    v_ref[...] = va
    v16_ref[...] = va.astype(BF16)

    zb = _dot(xb, wb_ref[...])
    qb_ref[...] = zb[:, :GLA_KW] * (GLA_DK ** -0.5)
    kb_ref[...] = zb[:, GLA_KW:2 * GLA_KW]
    vb_ref[...] = zb[:, 2 * GLA_KW:2 * GLA_KW + GLA_VW].astype(BF16)
    rb = zb[:, 2 * GLA_KW + GLA_VW:]
    rb_ref[...] = rb * _sigmoid(rb)
    ab = _dot(xb, wab_ref[...])
    gl = _dot(ab.astype(BF16), wal_ref[...]) + bal_ref[...]
    la_ref[...] = _log_sigmoid(gl) * (1.0 / GLA_GATE_TAU)


def _inproj(x, w, tm):
    n = x.shape[0]
    row = lambda c: pl.BlockSpec((tm, c), lambda i: (i, 0))
    full = lambda a: pl.BlockSpec(a.shape, lambda i: (0,) * a.ndim)
    ins = [x, w["g1"], w["wa"], w["wb"], w["wab"], w["gq"], w["gk"], w["hm"], w["wal"], w["bal"]]
    outs = [(SB_WIDTH, BF16), (SB_WIDTH, F32), (SB_WIDTH, BF16), (SB_WIDTH, F32), (SB_WIDTH, BF16),
            (GLA_KW, F32), (GLA_KW, F32), (GLA_VW, BF16), (GLA_VW, F32), (GLA_KW, F32)]
    return pl.pallas_call(
        _inproj_kernel,
        grid=(n // tm,),
        in_specs=[row(D_MODEL)] + [full(a) for a in ins[1:]],
        out_specs=[row(c) for c, _ in outs],
        out_shape=[jax.ShapeDtypeStruct((n, c), d) for c, d in outs],
        compiler_params=_cparams(("parallel",)),
        name="inproj",
    )(*ins)


def _sb_logs2(z2):
    l2 = jnp.log2(1.0 + jnp.exp2(-jnp.abs(z2)))
    lb2 = jnp.minimum(z2, 0.0) - l2
    return lb2, lb2 - z2


def _split_trunc(x):
    hi = pltpu.bitcast(pltpu.bitcast(x, jnp.uint32) & jnp.uint32(0xFFFF0000), F32)
    return hi.astype(BF16), (x - hi).astype(BF16)


def _sb_block(qh, k, v, bias, tri, carry, acc, mask):
    lb, lk = _sb_logs2(_dot_nt(qh, k) + bias)
    if mask is not None:
        lk = jnp.where(mask, lk, 0.0)
    hi, lo = _split_trunc(lk)
    r = _dot(jnp.concatenate([hi, lo], axis=1), tri)
    w = jnp.exp2(lb + r[:, :LANES] + carry)
    if mask is not None:
        w = jnp.where(mask, w, 0.0)
    acc = acc + _dot(w.astype(BF16), v)
    return carry + r[:, LANES:], acc


SB_TQ = 256


def _sb_prompt_kernel(bias_ref, q_ref, k_ref, v_ref, tri_ref, o_ref, carry_ref, acc_ref):
    pair = pl.program_id(1)
    qi = pl.program_id(2)
    tq = SB_TQ
    nsub = tq // LANES
    q2 = q_ref[0]
    tri = tri_ref[...]
    lane = lax.broadcasted_iota(jnp.int32, (tq, LANES), 1)
    rowi = lax.broadcasted_iota(jnp.int32, (tq, LANES), 0)
    low = lane < SB_HEAD_DIM
    zq = jnp.zeros_like(q2)
    qh = [jnp.where(low, q2, zq), jnp.where(low, zq, q2)]
    bias = [bias_ref[2 * pair], bias_ref[2 * pair + 1]]
    carry_ref[...] = jnp.zeros_like(carry_ref)
    acc_ref[...] = jnp.zeros_like(acc_ref)

    def visit(kstart, masked):
        state = [(carry_ref[hh], acc_ref[hh]) for hh in range(2)]
        for sub in reversed(range(nsub)):
            start = pl.multiple_of(kstart + sub * LANES, LANES)
            k = k_ref[0, pl.ds(start, LANES), :]
            v = v_ref[0, pl.ds(start, LANES), :]
            mask = (lane + sub * LANES) < rowi if masked else None
            state = [_sb_block(qh[hh], k, v, bias[hh], tri, state[hh][0], state[hh][1], mask)
                     for hh in range(2)]
        for hh in range(2):
            carry_ref[hh] = state[hh][0]
            acc_ref[hh] = state[hh][1]

    visit(qi * tq, True)

    @pl.loop(0, qi)
    def _(i):
        visit((qi - 1 - i) * tq, False)

    o_ref[0] = jnp.where(low, acc_ref[0], acc_ref[1]).astype(o_ref.dtype)


def _sb_prompt(q, k, v, bias, tri):
    b, t, _ = q.shape
    tq = SB_TQ
    pairs = SB_HEADS // 2
    grid_spec = pltpu.PrefetchScalarGridSpec(
        num_scalar_prefetch=1,
        grid=(b, pairs, t // tq),
        in_specs=[pl.BlockSpec((1, tq, LANES), lambda bi, p, qi, s: (bi, qi, p)),
                  pl.BlockSpec((1, t, LANES), lambda bi, p, qi, s: (bi, 0, p)),
                  pl.BlockSpec((1, t, LANES), lambda bi, p, qi, s: (bi, 0, p)),
                  pl.BlockSpec(tri.shape, lambda bi, p, qi, s: (0, 0))],
        out_specs=pl.BlockSpec((1, tq, LANES), lambda bi, p, qi, s: (bi, qi, p)),
        scratch_shapes=[pltpu.VMEM((2, tq, LANES), F32), pltpu.VMEM((2, tq, LANES), F32)],
    )
    return pl.pallas_call(
        _sb_prompt_kernel,
        grid_spec=grid_spec,
        out_shape=jax.ShapeDtypeStruct((b, t, SB_WIDTH), BF16),
        compiler_params=_cparams(("parallel", "parallel", "arbitrary")),
        name="sb_prompt",
    )(bias, q, k, v, tri)


PAGES_PER_STEP = 8


def _sb_decode_kernel(pt_ref, q_ref, bias_ref, tri_ref, pmat_ref, *refs):
    npg = PAGES_PER_STEP
    k_refs, v_refs = refs[:npg], refs[npg:2 * npg]
    o_ref, c_ref, acc_ref = refs[2 * npg:]
    step = pl.program_id(1)

    @pl.when(step == 0)
    def _():
        c_ref[...] = jnp.zeros_like(c_ref)
        acc_ref[...] = jnp.zeros_like(acc_ref)

    hd = SB_HEADS
    rows = PAGE_SIZE * hd
    ngr = rows // LANES
    q = q_ref[0]
    lane = lax.broadcasted_iota(jnp.int32, (hd, rows), 1)
    subl = lax.broadcasted_iota(jnp.int32, (hd, rows), 0)
    own = (lane % hd) == subl

    def page2d(ref):
        return ref[0, 0].reshape(rows, SB_HEAD_DIM).astype(BF16)

    zrows = [jnp.sum(jnp.where(own, _dot_nt(q, page2d(kr)), 0.0), axis=0, keepdims=True) for kr in k_refs]
    zp = jnp.concatenate(zrows, axis=0)
    z = jnp.concatenate([zp[:, g * LANES:(g + 1) * LANES] for g in range(ngr)], axis=0) + bias_ref[...]
    lb, lk = _sb_logs2(z)
    hi, lo = _split_trunc(lk)
    r = _dot(jnp.concatenate([hi, lo], axis=1), tri_ref[...])
    tot = r[:, LANES:]
    thi, tlo = _split_trunc(tot)
    outer = _dot(pmat_ref[...], jnp.concatenate([thi, tlo], axis=0))
    c = c_ref[...]
    w = jnp.exp2(lb + r[:, :LANES] + outer + jnp.concatenate([c] * ngr, axis=0))
    c_ref[...] = c + jnp.sum(tot, axis=0, keepdims=True)
    wp = jnp.concatenate([w[g * npg:(g + 1) * npg] for g in range(ngr)], axis=1)
    acc = acc_ref[...]
    for i in range(npg):
        wi = jnp.where(own, jnp.broadcast_to(wp[i:i + 1], (hd, rows)), 0.0).astype(BF16)
        acc = acc + _dot(wi, page2d(v_refs[i]))
    acc_ref[...] = acc

    @pl.when(step == pl.num_programs(1) - 1)
    def _():
        o_ref[0] = acc.astype(o_ref.dtype)


def _sb_decode(q, cache_k, cache_v, layer, page_table, bias_tile, tri, pmat):
    s, n_pages = page_table.shape
    npg = PAGES_PER_STEP
    steps = n_pages // npg

    def page_spec(i):
        return pl.BlockSpec((1, 1, PAGE_SIZE, SB_HEADS, SB_HEAD_DIM),
                            lambda si, j, pt: (layer, pt[si, n_pages - 1 - (npg * j + i)], 0, 0, 0))

    const = lambda a: pl.BlockSpec(a.shape, lambda si, j, pt: (0,) * a.ndim)
    qo = pl.BlockSpec((1, SB_HEADS, SB_HEAD_DIM), lambda si, j, pt: (si, 0, 0))
    grid_spec = pltpu.PrefetchScalarGridSpec(
        num_scalar_prefetch=1,
        grid=(s, steps),
        in_specs=[qo, const(bias_tile), const(tri), const(pmat)]
                 + [page_spec(i) for i in range(npg)] + [page_spec(i) for i in range(npg)],
        out_specs=qo,
        scratch_shapes=[pltpu.VMEM((SB_HEADS, LANES), F32), pltpu.VMEM((SB_HEADS, SB_HEAD_DIM), F32)],
    )
    return pl.pallas_call(
        _sb_decode_kernel,
        grid_spec=grid_spec,
        out_shape=jax.ShapeDtypeStruct((s, SB_HEADS, SB_HEAD_DIM), BF16),
        compiler_params=_cparams(("parallel", "arbitrary")),
        name="sb_decode",
    )(page_table, q, bias_tile, tri, pmat, *([cache_k] * npg), *([cache_v] * npg))


GLA_LEVELS = 6
GLA_MROWS = (2 * GLA_LEVELS + 2) * GLA_CHUNK


def _gla_constants():
    c = GLA_CHUNK
    j = np.arange(c)[None, :]
    t = np.arange(c)[:, None]
    blocks = [(j <= t)]
    masks = []
    qs, ks = [], []
    for lv in range(GLA_LEVELS):
        h = c >> (lv + 1)
        mid = (t // (2 * h)) * (2 * h) + h
        upper = (t % (2 * h)) >= h
        qs.append(upper & (j >= mid) & (j <= t))
        ks.append((~upper) & (j > t) & (j <= mid - 1))
        s = np.arange(c)[None, :]
        masks.append(((t // (2 * h)) == (s // (2 * h))) & upper & ((s % (2 * h)) < h))
    blocks += qs + ks + [(j > t)]
    masks.append(np.eye(c, dtype=bool))
    m = np.concatenate(blocks, axis=0).astype(np.float32)
    m2 = np.concatenate([m, m], axis=1)
    return jnp.asarray(m2, BF16), jnp.asarray(np.stack(masks).astype(np.float32))


def _gla_kernel(q_ref, k_ref, v_ref, la_ref, m_ref, lm_ref, o_ref, st_ref, s_scr):
    ci = pl.program_id(1)
    c = GLA_CHUNK

    @pl.when(ci == 0)
    def _():
        s_scr[...] = jnp.zeros_like(s_scr)

    q = q_ref[0]
    k = k_ref[0]
    v = v_ref[0]
    hi, lo = _split_bf16(la_ref[0])
    e = jnp.exp(_dot(m_ref[...], jnp.concatenate([hi, lo], axis=0)))
    blk = lambda i: e[i * c:(i + 1) * c]
    lane = lax.broadcasted_iota(jnp.int32, (c, GLA_KW), 1)
    q_in = (q * blk(0)).astype(BF16)
    k_st = (k * blk(2 * GLA_LEVELS + 1)).astype(BF16)
    qs = [(q * blk(1 + lv)).astype(BF16) for lv in range(GLA_LEVELS)] + [q.astype(BF16)]
    ks = [(k * blk(1 + GLA_LEVELS + lv)).astype(BF16) for lv in range(GLA_LEVELS)] + [k.astype(BF16)]
    s_old = s_scr[...]
    s_old16 = s_old.astype(BF16)
    zero16 = jnp.zeros((c, GLA_KW), BF16)
    upd = jnp.zeros_like(s_old)
    for h in range(GLA_HEADS):
        own = (lane // GLA_DK) == h
        vh = v[:, h * GLA_DV:(h + 1) * GLA_DV]
        scores = jnp.zeros((c, c), F32)
        for lv in range(GLA_LEVELS + 1):
            scores = scores + lm_ref[lv] * _dot_nt(jnp.where(own, qs[lv], zero16), ks[lv])
        o = _dot_nt(jnp.where(own, q_in, zero16), s_old16) + _dot(scores.astype(BF16), vh)
        o_ref[0, :, h * GLA_DV:(h + 1) * GLA_DV] = o
        upd = upd + _dot_tn(vh, jnp.where(own, k_st, zero16))
    s_new = s_old * e[c - 1:c, :] + upd
    s_scr[...] = s_new

    @pl.when(ci == pl.num_programs(1) - 1)
    def _():
        st_ref[0] = s_new


def _gla_prompt(q, k, v, la, m2, lmask):
    b, t, _ = q.shape
    c = GLA_CHUNK
    row = lambda w: pl.BlockSpec((1, c, w), lambda bi, ci: (bi, ci, 0))
    const = lambda a: pl.BlockSpec(a.shape, lambda bi, ci: (0,) * a.ndim)
    return pl.pallas_call(
        _gla_kernel,
        grid=(b, t // c),
        in_specs=[row(GLA_KW), row(GLA_KW), row(GLA_VW), row(GLA_KW), const(m2), const(lmask)],
        out_specs=[row(GLA_VW), pl.BlockSpec((1, GLA_DV, GLA_KW), lambda bi, ci: (bi, 0, 0))],
        out_shape=[jax.ShapeDtypeStruct((b, t, GLA_VW), F32),
                   jax.ShapeDtypeStruct((b, GLA_DV, GLA_KW), F32)],
        scratch_shapes=[pltpu.VMEM((GLA_DV, GLA_KW), F32)],
        compiler_params=_cparams(("parallel", "arbitrary")),
        name="gla_prompt",
    )(q, k, v, la, m2, lmask)


def _gla_step_kernel(q_ref, k_ref, la_ref, v_ref, s_ref, o_ref, sn_ref):
    a = jnp.exp(la_ref[0])
    s_new = a * s_ref[0] + k_ref[0] * v_ref[0]
    sn_ref[0] = s_new
    o_ref[0] = jnp.sum(q_ref[0] * s_new, axis=1, keepdims=True)


def _gla_step(q, k, la, v, s0):
    s = q.shape[0]
    col = pl.BlockSpec((1, GLA_HEADS, GLA_DK, 1), lambda i: (i, 0, 0, 0))
    vrow = pl.BlockSpec((1, GLA_HEADS, 1, GLA_DV), lambda i: (i, 0, 0, 0))
    st = pl.BlockSpec((1, GLA_HEADS, GLA_DK, GLA_DV), lambda i: (i, 0, 0, 0))
    return pl.pallas_call(
        _gla_step_kernel,
        grid=(s,),
        in_specs=[col, col, col, vrow, st],
        out_specs=[vrow, st],
        out_shape=[jax.ShapeDtypeStruct((s, GLA_HEADS, 1, GLA_DV), F32),
                   jax.ShapeDtypeStruct((s, GLA_HEADS, GLA_DK, GLA_DV), F32)],
        compiler_params=_cparams(("parallel",)),
        name="gla_step",
    )(q, k, la, v, s0)


def _merge_kernel(x_ref, ya_ref, ob_ref, rb_ref, g1_ref, wg_ref, bg_ref, gob_ref, wua_ref, wub_ref, wo_ref,
                  h_ref):
    x = x_ref[...]
    xb = _rms_rows(x, g1_ref[...]).astype(BF16)
    gates = _sigmoid(_dot(xb, wg_ref[...]) + bg_ref[...])
    ob = ob_ref[...]
    parts = []
    for h in range(GLA_HEADS):
        oh = ob[:, h * GLA_DV:(h + 1) * GLA_DV]
        parts.append(oh * lax.rsqrt(jnp.mean(oh * oh, axis=-1, keepdims=True) + RMS_EPS))
    yb = jnp.concatenate(parts, axis=1) * gob_ref[...] * rb_ref[...]
    m = (gates[:, :D_MODEL] * _dot(ya_ref[...], wua_ref[...])
         + gates[:, D_MODEL:] * _dot(yb.astype(BF16), wub_ref[...]))
    h_ref[...] = x + _dot(m.astype(BF16), wo_ref[...])


def _merge(x, ya, ob, rbs, w, tm):
    n = x.shape[0]
    row = lambda c: pl.BlockSpec((tm, c), lambda i: (i, 0))
    full = lambda a: pl.BlockSpec(a.shape, lambda i: (0,) * a.ndim)
    ws = [w["g1"], w["wg"], w["bg"], w["gob"], w["wua"], w["wub"], w["wo"]]
    return pl.pallas_call(
        _merge_kernel,
        grid=(n // tm,),
        in_specs=[row(D_MODEL), row(SB_WIDTH), row(GLA_VW), row(GLA_VW)] + [full(a) for a in ws],
        out_specs=row(D_MODEL),
        out_shape=jax.ShapeDtypeStruct((n, D_MODEL), F32),
        compiler_params=_cparams(("parallel",)),
        name="merge",
    )(x, ya, ob, rbs, *ws)


def _top_rows(x, n, ranked=False):
    rows = []
    rank = jnp.full(x.shape, float(n), F32) if ranked else None
    for a in range(n):
        m = jnp.max(x, axis=0, keepdims=True)
        rows.append(m)
        eq = x == m
        x = jnp.where(eq, NEG_INF, x)
        if ranked:
            rank = jnp.where(eq, float(a), rank)
    return (rows, rank) if ranked else rows


def _peer_prep_kernel(h_ref, g2_ref, wq_ref, sk_ref, xn_ref, r2_ref, p2_ref, nf_ref, p1_ref):
    k = PEER_TOPK
    xb = _rms_rows(h_ref[...], g2_ref[...]).astype(BF16)
    xn_ref[...] = xb
    qf = _dot(xb, wq_ref[...]).astype(BF16)
    tm = qf.shape[0]
    rowi = {r: lax.broadcasted_iota(jnp.int32, (r, tm), 0) for r in (8, k)}
    for h in range(PEER_HEADS):
        s = [_dot_nt(sk_ref[2 * h + c], qf[:, (2 * h + c) * PEER_HALF:(2 * h + c + 1) * PEER_HALF])
             for c in range(2)]
        v1, rank1 = _top_rows(s[0], k, ranked=True)
        v2, rank2 = _top_rows(s[1], k, ranked=True)
        v1a = jnp.concatenate(v1, axis=0)
        v2a = jnp.concatenate(v2, axis=0)
        cands = []
        for b in range(k):
            n_b = k // (b + 1)
            rows = k if b == 0 else 8
            cands.append(jnp.where(rowi[rows] < n_b, v1a[:rows] + v2[b], NEG_INF))
        top = _top_rows(jnp.concatenate(cands, axis=0), k)
        zsum = sum(jnp.exp(r - top[0]) for r in top)
        thr = top[k - 1]
        nf = jnp.zeros_like(rank1)
        for a in range(k):
            cnt = jnp.sum(jnp.where(v2a + v1[a] >= thr, 1.0, 0.0), axis=0, keepdims=True)
            nf = nf + jnp.where(rank1 == float(a), cnt, 0.0)
        r2_ref[h] = rank2.astype(BF16)
        p2_ref[h] = jnp.exp(s[1] - v2[0]).astype(BF16)
        nf_ref[h] = nf
        p1_ref[h] = jnp.exp(s[0] - v1[0]) / zsum


def _peer_prep(h1, w, tm):
    n = h1.shape[0]
    full = lambda a: pl.BlockSpec(a.shape, lambda i: (0,) * a.ndim)
    hk = pl.BlockSpec((PEER_HEADS, PEER_N_KEYS, tm), lambda i: (0, 0, i))
    hk_shape = lambda d: jax.ShapeDtypeStruct((PEER_HEADS, PEER_N_KEYS, n), d)
    return pl.pallas_call(
        _peer_prep_kernel,
        grid=(n // tm,),
        in_specs=[pl.BlockSpec((tm, D_MODEL), lambda i: (i, 0)), full(w["g2"]), full(w["wq"]), full(w["sk"])],
        out_specs=[pl.BlockSpec((tm, D_MODEL), lambda i: (i, 0)), hk, hk, hk, hk],
        out_shape=[jax.ShapeDtypeStruct((n, D_MODEL), BF16), hk_shape(BF16), hk_shape(BF16), hk_shape(F32),
                   hk_shape(F32)],
        compiler_params=_cparams(("parallel",)),
        name="peer_prep",
    )(h1, w["g2"], w["wq"], w["sk"])


PEER_ROWS_PER_STEP = 8


def _gelu(a):
    return 0.5 * a * (1.0 + lax.erf(a * (2.0 ** -0.5)))


def _peer_dense_kernel(xn_ref, u_ref, vt_ref, r2_ref, p2_ref, nf_ref, p1_ref,
                       h_ref, p_ref, g3_ref, wple_ref, wpg_ref, y_ref, acc_ref, wg_ref):
    j = pl.program_id(1)
    last = pl.num_programs(1) - 1
    tt = xn_ref.shape[0]
    nk = PEER_N_KEYS
    pk = 16

    @pl.when(j == 0)
    def _():
        acc_ref[...] = jnp.zeros_like(acc_ref)
        wg_ref[...] = jnp.zeros_like(wg_ref)

    acc_ref[...] += _dot(vt_ref[...], wg_ref[(j + 1) % 2])

    act = _gelu(_dot_nt(u_ref[...], xn_ref[...]))
    jb = jnp.minimum(j, last - 1)
    rows = pl.ds(pl.multiple_of(jb * PEER_ROWS_PER_STEP, PEER_ROWS_PER_STEP), PEER_ROWS_PER_STEP)
    slot = j % 2
    for tl in range(tt // LANES):
        ls = slice(tl * LANES, (tl + 1) * LANES)
        nft = [nf_ref[h, rows, ls] for h in range(PEER_HEADS)]
        p1t = [p1_ref[h, rows, ls] for h in range(PEER_HEADS)]
        for r in range(PEER_ROWS_PER_STEP):
            nb = [jnp.broadcast_to(nft[h][r:r + 1], (pk, LANES)).astype(BF16) for h in range(PEER_HEADS)]
            pb = [jnp.broadcast_to(p1t[h][r:r + 1], (pk, LANES)).astype(BF16) for h in range(PEER_HEADS)]
            for t in range(nk // pk):
                ks = slice(t * pk, (t + 1) * pk)
                wsum = jnp.zeros((pk, LANES), BF16)
                for h in range(PEER_HEADS):
                    sel = r2_ref[h, ks, ls] < nb[h]
                    wsum = wsum + jnp.where(sel, p2_ref[h, ks, ls], jnp.zeros((pk, LANES), BF16)) * pb[h]
                es = slice(r * nk + t * pk, r * nk + (t + 1) * pk)
                wg_ref[slot, es, ls] = wsum * act[es, ls].astype(BF16)

    @pl.when(j == last)
    def _():
        h2 = h_ref[...] + acc_ref[...].T
        xb = _rms_rows(h2, g3_ref[...]).astype(BF16)
        gate = _sigmoid(_dot(xb, wpg_ref[...]))
        y_ref[...] = h2 + _dot(p_ref[...].astype(BF16), wple_ref[...]) * gate


def _peer_dense(xn, r2, p2, nf, p1, h1, p, w, tt):
    n = xn.shape[0]
    eb = PEER_ROWS_PER_STEP * PEER_N_KEYS
    nj = w["u"].shape[0] // eb
    tok = lambda c: pl.BlockSpec((tt, c), lambda i, j: (i, 0))
    hk = pl.BlockSpec((PEER_HEADS, PEER_N_KEYS, tt), lambda i, j: (0, 0, i))
    full = lambda a: pl.BlockSpec(a.shape, lambda i, j: (0,) * a.ndim)
    return pl.pallas_call(
        _peer_dense_kernel,
        grid=(n // tt, nj + 1),
        in_specs=[tok(D_MODEL),
                  pl.BlockSpec((eb, D_MODEL), lambda i, j: (jnp.minimum(j, nj - 1), 0)),
                  pl.BlockSpec((D_MODEL, eb), lambda i, j: (0, jnp.maximum(j - 1, 0))),
                  hk, hk, hk, hk,
                  tok(D_MODEL), tok(PLE_DIM), full(w["g3"]), full(w["wple"]), full(w["wpg"])],
        out_specs=tok(D_MODEL),
        out_shape=jax.ShapeDtypeStruct((n, D_MODEL), F32),
        scratch_shapes=[pltpu.VMEM((D_MODEL, tt), F32), pltpu.VMEM((2, eb, tt), BF16)],
        compiler_params=_cparams(("parallel", "arbitrary")),
        name="peer_dense",
    )(xn, w["u"], w["vt"], r2, p2, nf, p1, h1, p, w["g3"], w["wple"], w["wpg"])


def _sb_constants():
    j = np.arange(LANES)[:, None]
    s = np.arange(LANES)[None, :]
    half = np.concatenate([(j > s), np.ones((LANES, LANES), bool)], axis=1)
    tri = np.concatenate([half, half], axis=0).astype(np.float32)
    return jnp.asarray(tri, BF16)


def _sb_decode_constants():
    npg, hd = PAGES_PER_STEP, SB_HEADS
    la = np.arange(LANES)
    same_head = (la[:, None] % hd) == (la[None, :] % hd)
    later_key = (la[:, None] // hd) > (la[None, :] // hd)
    half = np.concatenate([same_head & later_key, same_head], axis=1)
    tri = np.concatenate([half, half], axis=0).astype(np.float32)
    r = np.arange((PAGE_SIZE * hd // LANES) * npg)
    g, i = r // npg, r % npg
    p = ((i[None, :] == i[:, None]) & (g[None, :] > g[:, None])) | (i[None, :] < i[:, None])
    pmat = np.concatenate([p, p], axis=1).astype(np.float32)
    return jnp.asarray(tri, BF16), jnp.asarray(pmat, BF16)


def _layer_weights(i, norm1_g, w_in, b_gate, qa_norm_g, ka_norm_g, w_alpha, b_alpha, ob_norm_g, w_up_a,
                   w_up_b, w_o, norm2_g, w_peer_q, peer_sub_keys, expert_u, expert_v, norm3_g, w_ple,
                   w_ple_gate):
    win = w_in[i]
    c_a = 3 * SB_WIDTH
    c_b = c_a + 2 * GLA_KW + 2 * GLA_VW
    c_ab = c_b + GLA_GATE_RANK
    head = np.arange(SB_WIDTH) // SB_HEAD_DIM
    hm = (head[:, None] == head[None, :]).astype(np.float32) / SB_HEAD_DIM
    wab = jnp.zeros((D_MODEL, LANES), F32).at[:, :GLA_GATE_RANK].set(win[:, c_b:c_ab])
    wal = jnp.zeros((LANES, GLA_KW), F32).at[:GLA_GATE_RANK].set(w_alpha[i])
    return {
        "g1": norm1_g[i][None, :],
        "wa": win[:, :c_a].astype(BF16),
        "wb": win[:, c_a:c_b].astype(BF16),
        "wab": wab.astype(BF16),
        "gq": jnp.tile(qa_norm_g[i], SB_HEADS)[None, :],
        "gk": jnp.tile(ka_norm_g[i], SB_HEADS)[None, :],
        "hm": jnp.asarray(hm, BF16),
        "wal": wal.astype(BF16),
        "bal": b_alpha[i][None, :],
        "wg": win[:, c_ab:].astype(BF16),
        "bg": b_gate[i].reshape(1, 2 * D_MODEL),
        "gob": jnp.tile(ob_norm_g[i], GLA_HEADS)[None, :],
        "wua": w_up_a[i].astype(BF16),
        "wub": w_up_b[i].astype(BF16),
        "wo": w_o[i].astype(BF16),
        "g2": norm2_g[i][None, :],
        "wq": w_peer_q[i].astype(BF16),
        "sk": peer_sub_keys[i].reshape(2 * PEER_HEADS, PEER_N_KEYS, PEER_HALF).astype(BF16),
        "u": expert_u[i].astype(BF16),
        "vt": expert_v[i].astype(BF16).T,
        "g3": norm3_g[i][None, :],
        "wple": w_ple[i].astype(BF16),
        "wpg": w_ple_gate[i].astype(BF16),
    }


def _pad_rows(a, n):
    return jnp.pad(a, ((0, n - a.shape[0]),) + ((0, 0),) * (a.ndim - 1))


def kernel(x_prompt, x_sample, p_prompt, p_sample, cache_k, cache_v, state_gla, page_table, norm1_g, w_in,
           b_gate, qa_norm_g, ka_norm_g, sb_bias, w_alpha, b_alpha, ob_norm_g, w_up_a, w_up_b, w_o, norm2_g,
           w_peer_q, peer_sub_keys, expert_u, expert_v, norm3_g, w_ple, w_ple_gate):
    depth = w_in.shape[0]
    bsz, seq, _ = x_prompt.shape
    dec_b, dec_t, _ = x_sample.shape
    assert dec_t == 1 and seq % LANES == 0 and page_table.shape[1] % PAGES_PER_STEP == 0
    n_p = bsz * seq
    n_s = LANES
    tri = _sb_constants()
    dec_tri, dec_pmat = _sb_decode_constants()
    gla_m, gla_masks = _gla_constants()

    hp = x_prompt.reshape(n_p, D_MODEL)
    hs = _pad_rows(x_sample.reshape(dec_b, D_MODEL), n_s)
    outs = [[] for _ in range(6)]
    for i in range(depth):
        w = _layer_weights(i, norm1_g, w_in, b_gate, qa_norm_g, ka_norm_g, w_alpha, b_alpha, ob_norm_g,
                           w_up_a, w_up_b, w_o, norm2_g, w_peer_q, peer_sub_keys, expert_u, expert_v,
                           norm3_g, w_ple, w_ple_gate)
        bias = sb_bias[i].astype(F32) * LOG2E

        q, k, k16, v, v16, qb, kb, vb, rbs, la = _inproj(hp, w, 256)
        r3 = lambda a: a.reshape(bsz, seq, a.shape[-1])
        ya = _sb_prompt(r3(q), r3(k16), r3(v16), bias, tri)
        ob, st = _gla_prompt(r3(qb), r3(kb), r3(vb), r3(la), gla_m, gla_masks)
        h1 = _merge(hp, ya.reshape(n_p, SB_WIDTH), ob.reshape(n_p, GLA_VW), rbs, w, 256)
        xn, r2, p2, nf, p1 = _peer_prep(h1, w, 256)
        hp = _peer_dense(xn, r2, p2, nf, p1, h1, p_prompt[i].reshape(n_p, PLE_DIM), w, 512)
        outs[0].append(k.reshape(bsz, seq, SB_HEADS, SB_HEAD_DIM))
        outs[1].append(v.reshape(bsz, seq, SB_HEADS, SB_HEAD_DIM))
        outs[2].append(st.reshape(bsz, GLA_DV, GLA_HEADS, GLA_DK).transpose(0, 2, 3, 1))

        q, k, k16, v, v16, qb, kb, vb, rbs, la = _inproj(hs, w, n_s)
        bias_tile = jnp.broadcast_to(jnp.tile(bias, LANES // SB_HEADS)[None, :], dec_pmat.shape)
        ya = _sb_decode(q[:dec_b].reshape(dec_b, SB_HEADS, SB_HEAD_DIM), cache_k, cache_v, i, page_table,
                        bias_tile, dec_tri, dec_pmat)
        col = lambda a: a[:dec_b].reshape(dec_b, GLA_HEADS, GLA_DK, 1)
        ob, st = _gla_step(col(qb), col(kb), col(la),
                           vb[:dec_b].astype(F32).reshape(dec_b, GLA_HEADS, 1, GLA_DV), state_gla[i])
        h1 = _merge(hs, _pad_rows(ya.reshape(dec_b, SB_WIDTH), n_s), _pad_rows(ob.reshape(dec_b, GLA_VW), n_s),
                    rbs, w, n_s)
        xn, r2, p2, nf, p1 = _peer_prep(h1, w, n_s)
        hs = _peer_dense(xn, r2, p2, nf, p1, h1, _pad_rows(p_sample[i].reshape(dec_b, PLE_DIM), n_s), w, n_s)
        outs[3].append(k[:dec_b].reshape(dec_b, 1, SB_HEADS, SB_HEAD_DIM))
        outs[4].append(v[:dec_b].reshape(dec_b, 1, SB_HEADS, SB_HEAD_DIM))
        outs[5].append(st)

    return (hp.reshape(bsz, seq, D_MODEL), hs[:dec_b].reshape(dec_b, 1, D_MODEL),
            jnp.stack(outs[0]), jnp.stack(outs[1]), jnp.stack(outs[2]),
            jnp.stack(outs[3]), jnp.stack(outs[4]), jnp.stack(outs[5]))
```
